```python
import jax, jax.numpy as jnp
from jax import lax
import numpy as np

D_MODEL = 2048
BATCH = 4
SEQ = 8192
DEPTH = 4

N_MIXERS = 3
GRID_W = 64
BRANCH_W = D_MODEL
NORM_EPS = 1e-6

POOL_WINDOWS = (2, 4, 8, 16)
N_POOL_GROUPS = 4
POOL_GROUP_DIM = BRANCH_W // N_POOL_GROUPS

GLA_HEADS = 4
GLA_KEY_W = D_MODEL // 2
GLA_VAL_W = BRANCH_W
GLA_DK = GLA_KEY_W // GLA_HEADS
GLA_DV = GLA_VAL_W // GLA_HEADS
GLA_LOWRANK = 16
GLA_TAU = 16.0
GLA_CHUNK = 64

ATTN_HEAD_DIM = 128
ATTN_HEADS = BRANCH_W // ATTN_HEAD_DIM
ATTN_KV_HEADS = 4
ATTN_GROUP = ATTN_HEADS // ATTN_KV_HEADS
ATTN_Q_W = ATTN_HEADS * ATTN_HEAD_DIM
ATTN_KV_W = ATTN_KV_HEADS * ATTN_HEAD_DIM
ATTN_BLOCK = 128
ROPE_AXIS_DIM = ATTN_HEAD_DIM // 2
ROPE_THETA = 10000.0

kernel_name = "hybrid_pool_gla_gqa_encoder"


def rms_norm(x, eps=NORM_EPS):
    xf = x.astype(jnp.float32)
    return (xf * lax.rsqrt(jnp.mean(xf * xf, axis=-1, keepdims=True) + eps)).astype(x.dtype)


def ada_modulation(c, w, b):
    m = jnp.dot(jax.nn.silu(c), w) + b
    shift, scale, gate = jnp.split(m, 3, axis=-1)
    return shift[:, None], scale[:, None], gate[:, None]


def centred_window_mean(u, w):
    T = u.shape[1]
    csum = jnp.cumsum(u.astype(jnp.float32), axis=1)
    P = jnp.concatenate([jnp.zeros_like(csum[:, :1]), csum], axis=1)
    t = jnp.arange(T)
    lo = jnp.maximum(t - w // 2, 0)
    hi = jnp.minimum(t + w // 2, T)
    cnt = (hi - lo).astype(jnp.float32)
    return ((P[:, hi] - P[:, lo]) / cnt[None, :, None]).astype(u.dtype)


def pool_mixer(h, w_in, w_grp, scale, w_out):
    B, T, _ = h.shape
    u, g = jnp.split(h @ w_in, 2, axis=-1)
    ug = u.reshape(B, T, N_POOL_GROUPS, POOL_GROUP_DIM)
    pooled = jnp.stack(
        [centred_window_mean(ug[:, :, i], w) - ug[:, :, i] for i, w in enumerate(POOL_WINDOWS)],
        axis=2)
    y = jnp.einsum('btgc,gcd->btgd', pooled, w_grp).reshape(B, T, BRANCH_W) * scale
    return (y * jax.nn.silu(g)) @ w_out


def gla_chunked_scan(q, k, v, log_a, include_diag):
    B, T, H, DK = q.shape
    DV = v.shape[-1]
    nc = T // GLA_CHUNK

    def to_chunks(z):
        return z.astype(jnp.float32).reshape(B, nc, GLA_CHUNK, H, z.shape[-1]).transpose(1, 0, 3, 2, 4)

    qc, kc, vc, ac = to_chunks(q), to_chunks(k), to_chunks(v), to_chunks(log_a)
    mask = jnp.tril(jnp.ones((GLA_CHUNK, GLA_CHUNK), dtype=bool), 0 if include_diag else -1)

    def step(S, inp):
        qi, ki, vi, ai = inp
        b = jnp.cumsum(ai, axis=2)
        o_inter = jnp.einsum('bhcd,bhde->bhce', qi * jnp.exp(b), S)
        diff = b[:, :, :, None, :] - b[:, :, None, :, :]
        decay = jnp.exp(jnp.where(mask[:, :, None], diff, -jnp.inf))
        A = jnp.sum(qi[:, :, :, None, :] * ki[:, :, None, :, :] * decay, axis=-1)
        o_intra = jnp.einsum('bhij,bhje->bhie', A, vi)
        b_last = b[:, :, -1:, :]
        S_new = jnp.exp(b_last[:, :, 0, :])[..., None] * S + jnp.einsum(
            'bhjd,bhje->bhde', ki * jnp.exp(b_last - b), vi)
        return S_new, o_inter + o_intra

    S0 = jnp.zeros((B, H, DK, DV), jnp.float32)
    _, o = lax.scan(step, S0, (qc, kc, vc, ac))
    return o.transpose(1, 0, 3, 2, 4).reshape(B, T, H, DV)


def gla_mixer(h, w_in, fwd_w1, fwd_w2, fwd_b, bwd_w1, bwd_w2, bwd_b, norm_g, w_out):
    B, T, _ = h.shape
    q, k, v, g = jnp.split(h @ w_in, [GLA_KEY_W, 2 * GLA_KEY_W, 2 * GLA_KEY_W + GLA_VAL_W], axis=-1)
    q = (q * GLA_DK ** -0.5).reshape(B, T, GLA_HEADS, GLA_DK)
    k = k.reshape(B, T, GLA_HEADS, GLA_DK)
    v = v.reshape(B, T, GLA_HEADS, GLA_DV)

    def log_decay(w1, w2, b):
        z = ((h @ w1) @ w2 + b).astype(jnp.float32)
        return (jax.nn.log_sigmoid(z) / GLA_TAU).reshape(B, T, GLA_HEADS, GLA_DK)

    rev = lambda z: z[:, ::-1]
    o_fwd = gla_chunked_scan(q, k, v, log_decay(fwd_w1, fwd_w2, fwd_b), include_diag=True)
    o_bwd = rev(gla_chunked_scan(rev(q), rev(k), rev(v), rev(log_decay(bwd_w1, bwd_w2, bwd_b)),
                                 include_diag=False))
    o = (rms_norm(o_fwd + o_bwd) * norm_g).reshape(B, T, GLA_VAL_W).astype(h.dtype)
    return (o * jax.nn.silu(g)) @ w_out


def axial_rope_tables(T):
    rows = T // GRID_W
    t = jnp.arange(T)
    row = (t // GRID_W - rows // 2).astype(jnp.float32)
    col = (t % GRID_W - GRID_W // 2).astype(jnp.float32)
    inv = ROPE_THETA ** (-jnp.arange(0, ROPE_AXIS_DIM, 2, dtype=jnp.float32) / ROPE_AXIS_DIM)
    ang = jnp.concatenate([row[:, None] * inv, col[:, None] * inv], axis=-1)
    return jnp.cos(ang), jnp.sin(ang)


def apply_rope(x, cos, sin):
    shp = (cos.shape[0],) + (1,) * (x.ndim - 3) + (cos.shape[1],)
    cos = cos.reshape(shp).astype(x.dtype)
    sin = sin.reshape(shp).astype(x.dtype)
    xr = x.reshape(x.shape[:-1] + (-1, 2))
    x0, x1 = xr[..., 0], xr[..., 1]
    return jnp.stack([x0 * cos - x1 * sin, x0 * sin + x1 * cos], axis=-1).reshape(x.shape)


def attn_mixer(h, w_in, q_norm_g, k_norm_g, w_out):
    B, T, _ = h.shape
    q, k, v, g = jnp.split(h @ w_in, [ATTN_Q_W, ATTN_Q_W + ATTN_KV_W, ATTN_Q_W + 2 * ATTN_KV_W], axis=-1)
    q = rms_norm(q.reshape(B, T, ATTN_KV_HEADS, ATTN_GROUP, ATTN_HEAD_DIM)) * q_norm_g
    k = rms_norm(k.reshape(B, T, ATTN_KV_HEADS, ATTN_HEAD_DIM)) * k_norm_g
    v = v.reshape(B, T, ATTN_KV_HEADS, ATTN_HEAD_DIM)
    cos, sin = axial_rope_tables(T)
    q = apply_rope(q, cos, sin) * ATTN_HEAD_DIM ** -0.5
    k = apply_rope(k, cos, sin)
    nb = T // ATTN_BLOCK
    qb = q.reshape(B, nb, ATTN_BLOCK, ATTN_KV_HEADS, ATTN_GROUP, ATTN_HEAD_DIM).transpose(1, 0, 2, 3, 4, 5)

    def block(qi):
        s = jnp.einsum('bqkgd,bskd->bkgqs', qi, k, preferred_element_type=jnp.float32)
        p = jax.nn.softmax(s, axis=-1).astype(v.dtype)
        return jnp.einsum('bkgqs,bskd->bqkgd', p, v)

    o = lax.map(block, qb).transpose(1, 0, 2, 3, 4, 5).reshape(B, T, ATTN_Q_W)
    return (o * jax.nn.silu(g)) @ w_out


def setup_inputs(seed: int = 0) -> dict:
    key = jax.random.key(seed)
    ks = iter(jax.random.split(key, 40))
    n_pool = len(range(0, DEPTH, N_MIXERS))
    n_gla = len(range(1, DEPTH, N_MIXERS))
    n_attn = len(range(2, DEPTH, N_MIXERS))
    D = D_MODEL

    def nrm(shape, scale):
        return jax.random.normal(next(ks), shape, jnp.float32) * scale

    def gain(shape):
        return 1.0 + 0.1 * jax.random.normal(next(ks), shape, jnp.float32)

    return {
        "x": nrm((BATCH, SEQ, D), 1.0),
        "c": nrm((BATCH, D), 1.0),
        "w_mod": nrm((DEPTH, D, 3 * D), 0.5 * D ** -0.5),
        "b_mod": nrm((DEPTH, 3 * D), 0.02),
        "pool_w_in": nrm((n_pool, D, 2 * BRANCH_W), D ** -0.5),
        "pool_w_grp": nrm((n_pool, N_POOL_GROUPS, POOL_GROUP_DIM, POOL_GROUP_DIM), POOL_GROUP_DIM ** -0.5),
        "pool_scale": gain((n_pool, BRANCH_W)),
        "pool_w_out": nrm((n_pool, BRANCH_W, D), BRANCH_W ** -0.5),
        "gla_w_in": nrm((n_gla, D, 2 * GLA_KEY_W + 2 * GLA_VAL_W), D ** -0.5),
        "gla_fwd_w1": nrm((n_gla, D, GLA_LOWRANK), D ** -0.5),
        "gla_fwd_w2": nrm((n_gla, GLA_LOWRANK, GLA_KEY_W), GLA_LOWRANK ** -0.5),
        "gla_fwd_b": nrm((n_gla, GLA_KEY_W), 0.1),
        "gla_bwd_w1": nrm((n_gla, D, GLA_LOWRANK), D ** -0.5),
        "gla_bwd_w2": nrm((n_gla, GLA_LOWRANK, GLA_KEY_W), GLA_LOWRANK ** -0.5),
        "gla_bwd_b": nrm((n_gla, GLA_KEY_W), 0.1),
        "gla_norm_g": gain((n_gla, GLA_DV)),
        "gla_w_out": nrm((n_gla, GLA_VAL_W, D), GLA_VAL_W ** -0.5),
        "attn_w_in": nrm((n_attn, D, 2 * ATTN_Q_W + 2 * ATTN_KV_W), D ** -0.5),
        "attn_q_norm_g": gain((n_attn, ATTN_HEAD_DIM)),
        "attn_k_norm_g": gain((n_attn, ATTN_HEAD_DIM)),
        "attn_w_out": nrm((n_attn, ATTN_Q_W, D), ATTN_Q_W ** -0.5),
        "final_norm_g": gain((D,)),
    }


def reference(x, c, w_mod, b_mod, pool_w_in, pool_w_grp, pool_scale, pool_w_out,
              gla_w_in, gla_fwd_w1, gla_fwd_w2, gla_fwd_b, gla_bwd_w1, gla_bwd_w2, gla_bwd_b,
              gla_norm_g, gla_w_out, attn_w_in, attn_q_norm_g, attn_k_norm_g, attn_w_out,
              final_norm_g):
    for i in range(DEPTH):
        shift, scale, gate = ada_modulation(c, w_mod[i], b_mod[i])
        h = rms_norm(x) * (1.0 + scale) + shift
        kind, j = i % N_MIXERS, i // N_MIXERS
        if kind == 0:
            y = pool_mixer(h, pool_w_in[j], pool_w_grp[j], pool_scale[j], pool_w_out[j])
        elif kind == 1:
            y = gla_mixer(h, gla_w_in[j], gla_fwd_w1[j], gla_fwd_w2[j], gla_fwd_b[j],
                          gla_bwd_w1[j], gla_bwd_w2[j], gla_bwd_b[j], gla_norm_g[j], gla_w_out[j])
        else:
            y = attn_mixer(h, attn_w_in[j], attn_q_norm_g[j], attn_k_norm_g[j], attn_w_out[j])
        x = x + gate * y
    return rms_norm(x) * final_norm_g
```

```python
import functools

import jax
import jax.numpy as jnp
from jax import lax
from jax.experimental import pallas as pl
from jax.experimental.pallas import tpu as pltpu

F32 = jnp.float32
BF16 = jnp.bfloat16

N_MIXERS = 3
GRID_W = 64
NORM_EPS = 1e-6
POOL_WINDOWS = (2, 4, 8, 16)
GLA_HEADS = 4
GLA_TAU = 16.0
GLA_CHUNK = 64
ATTN_HEAD_DIM = 128
ATTN_KV_HEADS = 4
ROPE_THETA = 10000.0

LANES = 128
BF16_SUBLANES = 16
VMEM_LIMIT = 56 * 1024 * 1024
NEG_BIG = -1e30


def _tile(n, pref, mult):
    t = min(pref, n)
    t -= t % mult
    while t >= mult:
        if n % t == 0:
            return t
        t -= mult
    return n


def _silu(x):
    return x / (1.0 + jnp.exp(-x))


def _params(sem):
    return pltpu.CompilerParams(dimension_semantics=sem, vmem_limit_bytes=VMEM_LIMIT)


def _dot(a, b):
    return jnp.dot(a, b, preferred_element_type=F32)


def _dot_nt(a, b):
    return lax.dot_general(a, b, (((1,), (1,)), ((), ())), preferred_element_type=F32)


def _dot_tn(a, b):
    return lax.dot_general(a, b, (((0,), (0,)), ((), ())), preferred_element_type=F32)


def _dot_f32(a, b):
    return jnp.dot(a, b, preferred_element_type=F32, precision=lax.Precision.HIGHEST)


def _mod_kernel(c_ref, w_ref, b_ref, o_ref):
    sc = _silu(c_ref[...]).astype(BF16)
    o_ref[...] = _dot(sc, w_ref[...].astype(BF16)) + b_ref[...]


def _modulation(c, w_mod, b_mod):
    L, D, D3 = w_mod.shape
    B = c.shape[0]
    rows = -(-B // BF16_SUBLANES) * BF16_SUBLANES
    cp = jnp.pad(c, ((0, rows - B), (0, 0)))
    tn = _tile(D3, 1536, LANES)
    out = pl.pallas_call(
        _mod_kernel,
        out_shape=jax.ShapeDtypeStruct((L, rows, D3), F32),
        grid=(L, D3 // tn),
        in_specs=[
            pl.BlockSpec((rows, D), lambda l, j: (0, 0)),
            pl.BlockSpec((None, D, tn), lambda l, j: (l, 0, j)),
            pl.BlockSpec((None, 1, tn), lambda l, j: (l, 0, j)),
        ],
        out_specs=pl.BlockSpec((None, rows, tn), lambda l, j: (l, 0, j)),
        compiler_params=_params(("parallel", "parallel")),
        name="adaln_modulation",
    )(cp, w_mod, b_mod.reshape(L, 1, D3))
    return out[:, :B]


def _mod_spec(layer, part, tm, T, D, nargs):
    if nargs == 1:
        return pl.BlockSpec((None, None, None, 1, D),
                            lambda i: (layer, (i * tm) // T, part, 0, 0))
    return pl.BlockSpec((None, None, None, 1, D),
                        lambda i, j: (layer, (i * tm) // T, part, 0, 0))


def _norm_mod(x_ref, sh_ref, sc_ref, h_ref):
    xf = x_ref[...]
    ms = jnp.mean(xf * xf, axis=-1, keepdims=True)
    h = xf * lax.rsqrt(ms + NORM_EPS) * (1.0 + sc_ref[...]) + sh_ref[...]
    h_ref[...] = h.astype(BF16)


def _inproj_plain_kernel(x_ref, sh_ref, sc_ref, w_ref, o_ref, h_ref):
    @pl.when(pl.program_id(1) == 0)
    def _():
        _norm_mod(x_ref, sh_ref, sc_ref, h_ref)

    o_ref[...] = _dot(h_ref[...], w_ref[...]).astype(o_ref.dtype)


def _inproj_gla_kernel(x_ref, sh_ref, sc_ref, w_ref, w1_ref, o_ref, r_ref, h_ref, *,
                       q_tiles, q_scale):
    j = pl.program_id(1)

    @pl.when(j == 0)
    def _():
        _norm_mod(x_ref, sh_ref, sc_ref, h_ref)
        r_ref[...] = _dot(h_ref[...], w1_ref[...])

    s = jnp.where(j < q_tiles, q_scale, 1.0).astype(F32)
    o_ref[...] = (_dot(h_ref[...], w_ref[...]) * s).astype(o_ref.dtype)


def _norm_rope(acc, gain, cos, sin, post_scale):
    tn = acc.shape[1]
    lane = lax.broadcasted_iota(jnp.int32, (1, ATTN_HEAD_DIM), 1)
    even = (lane % 2) == 0
    outs = []
    for s in range(tn // ATTN_HEAD_DIM):
        a = acc[:, s * ATTN_HEAD_DIM:(s + 1) * ATTN_HEAD_DIM]
        ms = jnp.mean(a * a, axis=-1, keepdims=True)
        a = a * lax.rsqrt(ms + NORM_EPS) * gain
        partner = jnp.where(even, pltpu.roll(a, ATTN_HEAD_DIM - 1, 1), pltpu.roll(a, 1, 1))
        outs.append((a * cos + partner * sin) * post_scale)
    return jnp.concatenate(outs, axis=1)


def _inproj_attn_kernel(x_ref, sh_ref, sc_ref, w_ref, gq_ref, gk_ref, cos_ref, sin_ref,
                        o_ref, h_ref, *, q_tiles, k_lo, k_hi, q_scale):
    j = pl.program_id(1)

    @pl.when(j == 0)
    def _():
        _norm_mod(x_ref, sh_ref, sc_ref, h_ref)

    acc = _dot(h_ref[...], w_ref[...])
    is_q = j < q_tiles
    is_k = jnp.logical_and(j >= k_lo, j < k_hi)

    @pl.when(is_q)
    def _():
        o_ref[...] = _norm_rope(acc, gq_ref[...], cos_ref[...], sin_ref[...],
                                q_scale).astype(o_ref.dtype)

    @pl.when(is_k)
    def _():
        o_ref[...] = _norm_rope(acc, gk_ref[...], cos_ref[...], sin_ref[...],
                                1.0).astype(o_ref.dtype)

    @pl.when(jnp.logical_not(jnp.logical_or(is_q, is_k)))
    def _():
        o_ref[...] = acc.astype(o_ref.dtype)


def _inproj(x, mod5, layer, w, T, *, mode, extra=None):
    N, D = x.shape
    W = w.shape[1]
    tm = _tile(T, 1024, 8)
    if mode == "attn":
        tn = _tile(extra["kv_w"], 512, LANES)
    elif mode == "gla":
        tn = _tile(extra["key_w"], 1024, LANES)
    else:
        tn = _tile(W, 1024, LANES)
    grid = (N // tm, W // tn)
    in_specs = [
        pl.BlockSpec((tm, D), lambda i, j: (i, 0)),
        _mod_spec(layer, 0, tm, T, D, 2),
        _mod_spec(layer, 1, tm, T, D, 2),
        pl.BlockSpec((D, tn), lambda i, j: (0, j)),
    ]
    args = [x, mod5, mod5, w]
    out_shape = jax.ShapeDtypeStruct((N, W), BF16)
    out_specs = pl.BlockSpec((tm, tn), lambda i, j: (i, j))
    scratch = [pltpu.VMEM((tm, D), BF16)]
    if mode == "plain":
        body = _inproj_plain_kernel
    elif mode == "gla":
        w1 = extra["w1"]
        body = functools.partial(_inproj_gla_kernel, q_tiles=extra["key_w"] // tn,
                                 q_scale=extra["q_scale"])
        assert extra["key_w"] % tn == 0
        in_specs.append(pl.BlockSpec(w1.shape, lambda i, j: (0, 0)))
        args.append(w1)
        out_shape = (out_shape, jax.ShapeDtypeStruct((N, w1.shape[1]), F32))
        out_specs = (out_specs, pl.BlockSpec((tm, w1.shape[1]), lambda i, j: (i, 0)))
    else:
        q_w, g_w, kv_w = extra["q_w"], extra["g_w"], extra["kv_w"]
        assert q_w % tn == 0 and g_w % tn == 0 and kv_w % tn == 0
        body = functools.partial(
            _inproj_attn_kernel, q_tiles=q_w // tn, k_lo=(q_w + g_w) // tn,
            k_hi=(q_w + g_w + kv_w) // tn, q_scale=extra["q_scale"])
        tpb = T // tm
        in_specs += [
            pl.BlockSpec((1, ATTN_HEAD_DIM), lambda i, j: (0, 0)),
            pl.BlockSpec((1, ATTN_HEAD_DIM), lambda i, j: (0, 0)),
            pl.BlockSpec((tm, ATTN_HEAD_DIM), lambda i, j: (i % tpb, 0)),
            pl.BlockSpec((tm, ATTN_HEAD_DIM), lambda i, j: (i % tpb, 0)),
        ]
        args += [extra["gq"], extra["gk"], extra["cos"], extra["sin"]]
    return pl.pallas_call(
        body, out_shape=out_shape, grid=grid, in_specs=in_specs, out_specs=out_specs,
        scratch_shapes=scratch, compiler_params=_params(("parallel", "arbitrary")),
        name="inproj_" + mode,
    )(*args)


def _finish(y_ref, wout_ref, x_ref, gate_ref, fg_ref, o_ref, final):
    out = x_ref[...] + gate_ref[...] * _dot(y_ref[...], wout_ref[...])
    if final:
        ms = jnp.mean(out * out, axis=-1, keepdims=True)
        out = out * lax.rsqrt(ms + NORM_EPS) * fg_ref[...]
    o_ref[...] = out


def _pool_out_kernel(u_ref, up_ref, un_ref, g_ref, x_ref, gate_ref, wg_ref, ps_ref,
                     wout_ref, fg_ref, o_ref, ext_ref, y_ref, *, T, final):
    tm = u_ref.shape[0]
    hb = up_ref.shape[0]
    cg = wg_ref.shape[1]
    tiles_per_seq = T // tm
    ti = pl.program_id(0) % tiles_per_seq
    first = ti == 0
    last = ti == tiles_per_seq - 1
    pos = ti * tm + lax.broadcasted_iota(jnp.int32, (tm, 1), 0)
    for gi, w in enumerate(POOL_WINDOWS):
        cs = slice(gi * cg, (gi + 1) * cg)
        um = u_ref[:, cs].astype(F32)
        ext_ref[0:hb, :] = jnp.where(first, 0.0, up_ref[:, cs].astype(F32))
        ext_ref[hb:hb + tm, :] = um
        ext_ref[hb + tm:hb + tm + hb, :] = jnp.where(last, 0.0, un_ref[:, cs].astype(F32))
        wsum = ext_ref[hb - w // 2:hb - w // 2 + tm, :]
        for k in range(1 - w // 2, w // 2):
            wsum = wsum + ext_ref[hb + k:hb + k + tm, :]
        cnt = (jnp.minimum(pos + w // 2, T) - jnp.maximum(pos - w // 2, 0)).astype(F32)
        pooled = wsum / cnt - um
        yg = _dot(pooled.astype(BF16), wg_ref[gi]) * ps_ref[:, cs]
        y_ref[:, cs] = (yg * _silu(g_ref[:, cs].astype(F32))).astype(BF16)
    _finish(y_ref, wout_ref, x_ref, gate_ref, fg_ref, o_ref, final)


def _gla_out_kernel(of_ref, ob_ref, g_ref, x_ref, gate_ref, ng_ref, wout_ref, fg_ref,
                    o_ref, y_ref, *, final):
    dv = ng_ref.shape[1]
    for h in range(GLA_HEADS):
        hs = slice(h * dv, (h + 1) * dv)
        o = of_ref[:, hs].astype(F32) + ob_ref[:, hs].astype(F32)
        ms = jnp.mean(o * o, axis=-1, keepdims=True)
        o = o * lax.rsqrt(ms + NORM_EPS) * ng_ref[...]
        y_ref[:, hs] = (o * _silu(g_ref[:, hs].astype(F32))).astype(BF16)
    _finish(y_ref, wout_ref, x_ref, gate_ref, fg_ref, o_ref, final)


def _attn_out_kernel(a_ref, g_ref, x_ref, gate_ref, wout_ref, fg_ref, o_ref, y_ref, *, final):
    y_ref[...] = (a_ref[...].astype(F32) * _silu(g_ref[...].astype(F32))).astype(BF16)
    _finish(y_ref, wout_ref, x_ref, gate_ref, fg_ref, o_ref, final)


def _const_spec(shape):
    nd = len(shape)
    return pl.BlockSpec(shape, lambda i: (0,) * nd)


def _out_call(body, name, pre_args, pre_specs, post_args, post_specs, x, mod5, layer, T,
              w_out, final_g, tm, scratch):
    N, D = x.shape
    bw = w_out.shape[0]
    in_specs = (pre_specs
                + [pl.BlockSpec((tm, D), lambda i: (i, 0)), _mod_spec(layer, 2, tm, T, D, 1)]
                + post_specs + [_const_spec((bw, D)), _const_spec((1, D))])
    args = pre_args + [x, mod5] + post_args + [w_out, final_g.reshape(1, D)]
    return pl.pallas_call(
        body, out_shape=jax.ShapeDtypeStruct((N, D), F32), grid=(N // tm,),
        in_specs=in_specs, out_specs=pl.BlockSpec((tm, D), lambda i: (i, 0)),
        scratch_shapes=scratch + [pltpu.VMEM((tm, bw), BF16)],
        compiler_params=_params(("parallel",)), name=name,
    )(*args)


def _pool_out(ug, x, mod5, layer, T, w_grp, scale, w_out, final_g, final):
    N, D = x.shape
    bw = w_out.shape[0]
    cg = w_grp.shape[1]
    hb = BF16_SUBLANES
    assert max(POOL_WINDOWS) // 2 <= hb
    tm = _tile(T, 256, hb)
    r = tm // hb
    nhb = N // hb
    pre_specs = [
        pl.BlockSpec((tm, bw), lambda i: (i, 0)),
        pl.BlockSpec((hb, bw), lambda i: (jnp.maximum(i * r - 1, 0), 0)),
        pl.BlockSpec((hb, bw), lambda i: (jnp.minimum((i + 1) * r, nhb - 1), 0)),
        pl.BlockSpec((tm, bw), lambda i: (i, 1)),
    ]
    post_specs = [_const_spec(w_grp.shape), _const_spec((1, bw))]
    body = functools.partial(_pool_out_kernel, T=T, final=final)
    return _out_call(body, "pool_out", [ug, ug, ug, ug], pre_specs,
                     [w_grp, scale.reshape(1, bw)], post_specs, x, mod5, layer, T, w_out,
                     final_g, tm, [pltpu.VMEM((tm + 2 * hb, cg), F32)])


def _gla_out(o_f, o_b, qkvg, x, mod5, layer, T, norm_g, w_out, final_g, final):
    N, D = x.shape
    bw = w_out.shape[0]
    tm = _tile(T, 512, 8)
    pre_specs = [
        pl.BlockSpec((tm, bw), lambda i: (i, 0)),
        pl.BlockSpec((tm, bw), lambda i: (i, 0)),
        pl.BlockSpec((tm, bw), lambda i: (i, 2)),
    ]
    dv = norm_g.shape[0]
    body = functools.partial(_gla_out_kernel, final=final)
    return _out_call(body, "gla_out", [o_f, o_b, qkvg], pre_specs, [norm_g.reshape(1, dv)],
                     [_const_spec((1, dv))], x, mod5, layer, T, w_out, final_g, tm, [])


def _attn_out(a, qgkv, x, mod5, layer, T, w_out, final_g, final):
    N, D = x.shape
    bw = w_out.shape[0]
    tm = _tile(T, 512, 8)
    pre_specs = [
        pl.BlockSpec((tm, bw), lambda i: (i, 0)),
        pl.BlockSpec((tm, bw), lambda i: (i, 1)),
    ]
    body = functools.partial(_attn_out_kernel, final=final)
    return _out_call(body, "attn_out", [a, qgkv], pre_specs, [], [], x, mod5, layer, T,
                     w_out, final_g, tm, [])


def _block_ref_rows(cum, m, ref_local):
    C, W = cum.shape
    blk = 2 * m
    if blk >= 8:
        x3 = cum.reshape(C // blk, blk, W)
        r = x3[:, ref_local:ref_local + 1, :]
        return jnp.broadcast_to(r, x3.shape).reshape(C, W)
    x3 = cum.reshape(C // 8, 8, W)
    sub = lax.broadcasted_iota(jnp.int32, (1, 8, 1), 1)
    out = None
    for p in range(8 // blk):
        r = jnp.broadcast_to(x3[:, p * blk + ref_local:p * blk + ref_local + 1, :], x3.shape)
        out = r if out is None else jnp.where(sub >= p * blk, r, out)
    return out.reshape(C, W)


def _gla_chunk(qk_ref, v_ref, r_ref, w2_ref, b_ref, st_ref, o_ref, off, fwd):
    C = GLA_CHUNK
    H = GLA_HEADS
    kw = b_ref.shape[1]
    dk = kw // H
    dv = v_ref.shape[1] // H
    rows = pl.ds(off, C)
    row = lax.broadcasted_iota(jnp.int32, (C, 1), 0)
    rr = lax.broadcasted_iota(jnp.int32, (C, C), 0)
    cc = lax.broadcasted_iota(jnp.int32, (C, C), 1)

    z = _dot_f32(r_ref[rows, :], w2_ref[...]) + b_ref[...]
    la = (jnp.minimum(z, 0.0) - jnp.log(1.0 + jnp.exp(-jnp.abs(z)))) * (1.0 / GLA_TAU)
    tri = (rr >= cc) if fwd else (cc >= rr)
    cum = _dot_f32(tri.astype(F32), la)
    tot = cum[C - 1:C, :] if fwd else cum[0:1, :]

    q = qk_ref[rows, 0:kw].astype(F32)
    k = qk_ref[rows, kw:2 * kw].astype(F32)
    q_inter = (q * jnp.exp(cum)).astype(BF16)
    k_dec = (k * jnp.exp(tot - cum)).astype(BF16)
    dec = jnp.exp(tot)

    amat = [jnp.zeros((C, C), F32) for _ in range(H)]
    m = C // 2
    while m >= 1:
        ref = _block_ref_rows(cum, m, m - 1 if fwd else m)
        upper = ((row // m) % 2) == 1
        qmask = upper if fwd else jnp.logical_not(upper)
        qm = (q * jnp.exp(jnp.where(qmask, cum - ref, NEG_BIG))).astype(BF16)
        km = (k * jnp.exp(jnp.where(qmask, NEG_BIG, ref - cum))).astype(BF16)
        same = (rr // (2 * m)) == (cc // (2 * m))
        for h in range(H):
            hs = slice(h * dk, (h + 1) * dk)
            amat[h] = amat[h] + jnp.where(same, _dot_nt(qm[:, hs], km[:, hs]), 0.0)
        m //= 2
    if fwd:
        qb = q.astype(BF16)
        kb = k.astype(BF16)
        for h in range(H):
            hs = slice(h * dk, (h + 1) * dk)
            amat[h] = amat[h] + jnp.where(rr == cc, _dot_nt(qb[:, hs], kb[:, hs]), 0.0)

    for h in range(H):
        hs = slice(h * dk, (h + 1) * dk)
        vs = slice(h * dv, (h + 1) * dv)
        v = v_ref[rows, vs]
        st = st_ref[h]
        o = _dot_nt(q_inter[:, hs], st.astype(BF16)) + _dot(amat[h].astype(BF16), v)
        o_ref[rows, vs] = o.astype(o_ref.dtype)
        st_ref[h] = st * dec[:, hs] + _dot_tn(v, k_dec[:, hs])


def _gla_kernel(qkf_ref, vf_ref, rf_ref, qkb_ref, vb_ref, rb_ref, w2f_ref, w2b_ref,
                bf_ref, bb_ref, of_ref, ob_ref, sf_ref, sb_ref):
    @pl.when(pl.program_id(1) == 0)
    def _():
        sf_ref[...] = jnp.zeros_like(sf_ref)
        sb_ref[...] = jnp.zeros_like(sb_ref)

    nsub = qkf_ref.shape[0] // GLA_CHUNK

    def body(ci, carry):
        off_f = pl.multiple_of(ci * GLA_CHUNK, GLA_CHUNK)
        off_b = pl.multiple_of((nsub - 1 - ci) * GLA_CHUNK, GLA_CHUNK)
        _gla_chunk(qkf_ref, vf_ref, rf_ref, w2f_ref, bf_ref, sf_ref, of_ref, off_f, True)
        _gla_chunk(qkb_ref, vb_ref, rb_ref, w2b_ref, bb_ref, sb_ref, ob_ref, off_b, False)
        return carry

    lax.fori_loop(0, nsub, body, 0)


def _gla_core(qkvg, r, w2f, w2b, bias_f, bias_b, B, T, D):
    kw = bias_f.shape[1]
    H = GLA_HEADS
    tsup = _tile(T, 256, GLA_CHUNK)
    ns = T // tsup
    rw = r.shape[-1]
    qkvg3 = qkvg.reshape(B, T, qkvg.shape[-1])
    r3 = r.reshape(B, T, rw)
    assert 2 * kw == D
    fw = lambda b, s: (b, s, 0)
    fw1 = lambda b, s: (b, s, 1)
    bw = lambda b, s: (b, ns - 1 - s, 0)
    bw1 = lambda b, s: (b, ns - 1 - s, 1)
    cst = lambda b, s: (0, 0)
    in_specs = [
        pl.BlockSpec((None, tsup, D), fw), pl.BlockSpec((None, tsup, D), fw1),
        pl.BlockSpec((None, tsup, rw), fw),
        pl.BlockSpec((None, tsup, D), bw), pl.BlockSpec((None, tsup, D), bw1),
        pl.BlockSpec((None, tsup, rw), bw),
        pl.BlockSpec((rw, kw), cst), pl.BlockSpec((rw, kw), cst),
        pl.BlockSpec((1, kw), cst), pl.BlockSpec((1, kw), cst),
    ]
    out_shape = (jax.ShapeDtypeStruct((B, T, D), BF16), jax.ShapeDtypeStruct((B, T, D), BF16))
    out_specs = (pl.BlockSpec((None, tsup, D), fw), pl.BlockSpec((None, tsup, D), bw))
    st = pltpu.VMEM((H, D // H, kw // H), F32)
    o_f, o_b = pl.pallas_call(
        _gla_kernel, out_shape=out_shape, grid=(B, ns), in_specs=in_specs,
        out_specs=out_specs, scratch_shapes=[st, st],
        compiler_params=_params(("parallel", "arbitrary")), name="gla_scan",
    )(qkvg3, qkvg3, r3, qkvg3, qkvg3, r3, w2f, w2b, bias_f, bias_b)
    return o_f.reshape(B * T, D), o_b.reshape(B * T, D)


def _attn_kernel(q_ref, k_ref, v_ref, o_ref, m_ref, l_ref, acc_ref, *, tk, group):
    tq = q_ref.shape[0]
    hd = ATTN_HEAD_DIM
    nk = k_ref.shape[0] // tk
    qs = jnp.concatenate([q_ref[:, g * hd:(g + 1) * hd] for g in range(group)], axis=0)
    m_ref[...] = jnp.full(m_ref.shape, -jnp.inf, F32)
    l_ref[...] = jnp.zeros_like(l_ref)
    acc_ref[...] = jnp.zeros_like(acc_ref)

    def body(j, carry):
        rows = pl.ds(pl.multiple_of(j * tk, tk), tk)
        s = _dot_nt(qs, k_ref[rows, :])
        m_prev = m_ref[...]
        m_new = jnp.maximum(m_prev, jnp.max(s, axis=-1, keepdims=True))
        alpha = jnp.exp(m_prev - m_new)
        p = jnp.exp(s - m_new)
        l_ref[...] = alpha * l_ref[...] + jnp.sum(p, axis=-1, keepdims=True)
        acc_ref[...] = alpha * acc_ref[...] + _dot(p.astype(BF16), v_ref[rows, :])
        m_ref[...] = m_new
        return carry

    lax.fori_loop(0, nk, body, 0)
    o = acc_ref[...] / l_ref[...]
    o_ref[...] = jnp.concatenate([o[g * tq:(g + 1) * tq, :] for g in range(group)],
                                 axis=1).astype(o_ref.dtype)


def _attention(qgkv, B, T, q_w, g_w):
    hd = ATTN_HEAD_DIM
    kvh = ATTN_KV_HEADS
    group = q_w // (kvh * hd)
    gw = group * hd
    tq = _tile(T, 128, 8)
    tk = _tile(T, 512, LANES)
    k_blk = (q_w + g_w) // hd
    v_blk = k_blk + kvh
    a3 = qgkv.reshape(B, T, qgkv.shape[-1])
    rows = group * tq
    out = pl.pallas_call(
        functools.partial(_attn_kernel, tk=tk, group=group),
        out_shape=jax.ShapeDtypeStruct((B, T, q_w), BF16),
        grid=(B, kvh, T // tq),
        in_specs=[
            pl.BlockSpec((None, tq, gw), lambda b, h, i: (b, i, h)),
            pl.BlockSpec((None, T, hd), lambda b, h, i: (b, 0, k_blk + h)),
            pl.BlockSpec((None, T, hd), lambda b, h, i: (b, 0, v_blk + h)),
        ],
        out_specs=pl.BlockSpec((None, tq, gw), lambda b, h, i: (b, i, h)),
        scratch_shapes=[pltpu.VMEM((rows, 1), F32), pltpu.VMEM((rows, 1), F32),
                        pltpu.VMEM((rows, hd), F32)],
        compiler_params=_params(("parallel", "parallel", "arbitrary")),
        name="gqa_flash",
    )(a3, a3, a3)
    return out.reshape(B * T, q_w)


def _rope_tables(T):
    rows = T // GRID_W
    t = jnp.arange(T)
    row = (t // GRID_W - rows // 2).astype(F32)
    col = (t % GRID_W - GRID_W // 2).astype(F32)
    axis = ATTN_HEAD_DIM // 2
    inv = ROPE_THETA ** (-jnp.arange(0, axis, 2, dtype=F32) / axis)
    ang = jnp.concatenate([row[:, None] * inv, col[:, None] * inv], axis=-1)
    cos = jnp.repeat(jnp.cos(ang), 2, axis=-1)
    sin = jnp.repeat(jnp.sin(ang), 2, axis=-1)
    sign = jnp.where(jnp.arange(ATTN_HEAD_DIM) % 2 == 0, -1.0, 1.0).astype(F32)
    return cos, sin * sign


def kernel(x, c, w_mod, b_mod, pool_w_in, pool_w_grp, pool_scale, pool_w_out, gla_w_in, gla_fwd_w1, gla_fwd_w2, gla_fwd_b, gla_bwd_w1, gla_bwd_w2, gla_bwd_b, gla_norm_g, gla_w_out, attn_w_in, attn_q_norm_g, attn_k_norm_g, attn_w_out, final_norm_g):
    B, T, D = x.shape
    depth = w_mod.shape[0]
    mod5 = _modulation(c, w_mod, b_mod).reshape(depth, B, 3, 1, D)
    xf = x.reshape(B * T, D)
    for i in range(depth):
        kind, j = i % N_MIXERS, i // N_MIXERS
        final = i == depth - 1
        if kind == 0:
            ug = _inproj(xf, mod5, i, pool_w_in[j].astype(BF16), T, mode="plain")
            xf = _pool_out(ug, xf, mod5, i, T, pool_w_grp[j].astype(BF16), pool_scale[j],
                           pool_w_out[j].astype(BF16), final_norm_g, final)
        elif kind == 1:
            key_w = gla_fwd_b.shape[1]
            lr = gla_fwd_w1.shape[2]
            w1 = jnp.concatenate([gla_fwd_w1[j], gla_bwd_w1[j]], axis=1)
            w1 = jnp.pad(w1, ((0, 0), (0, LANES - 2 * lr))).astype(BF16)
            w2f = jnp.pad(gla_fwd_w2[j], ((0, LANES - lr), (0, 0)))
            w2b = jnp.pad(gla_bwd_w2[j], ((lr, LANES - 2 * lr), (0, 0)))
            extra = dict(w1=w1, key_w=key_w, q_scale=float(key_w // GLA_HEADS) ** -0.5)
            qkvg, r = _inproj(xf, mod5, i, gla_w_in[j].astype(BF16), T, mode="gla",
                              extra=extra)
            o_f, o_b = _gla_core(qkvg, r, w2f, w2b, gla_fwd_b[j].reshape(1, key_w),
                                 gla_bwd_b[j].reshape(1, key_w), B, T, D)
            xf = _gla_out(o_f, o_b, qkvg, xf, mod5, i, T, gla_norm_g[j],
                          gla_w_out[j].astype(BF16), final_norm_g, final)
        else:
            q_w = attn_w_out.shape[1]
            kv_w = ATTN_KV_HEADS * ATTN_HEAD_DIM
            w = attn_w_in[j]
            w = jnp.concatenate([w[:, :q_w], w[:, q_w + 2 * kv_w:], w[:, q_w:q_w + 2 * kv_w]],
                                axis=1).astype(BF16)
            cos, sin = _rope_tables(T)
            extra = dict(q_w=q_w, g_w=D, kv_w=kv_w, q_scale=float(ATTN_HEAD_DIM) ** -0.5,
                         gq=attn_q_norm_g[j].reshape(1, -1), gk=attn_k_norm_g[j].reshape(1, -1),
                         cos=cos, sin=sin)
            qgkv = _inproj(xf, mod5, i, w, T, mode="attn", extra=extra)
            a = _attention(qgkv, B, T, q_w, D)
            xf = _attn_out(a, qgkv, xf, mod5, i, T, attn_w_out[j].astype(BF16),
                           final_norm_g, final)
    return xf.reshape(B, T, D)
```

```python
import functools

import jax
import jax.numpy as jnp
from jax import lax
from jax.experimental import pallas as pl
from jax.experimental.pallas import tpu as pltpu

F32 = jnp.float32
BF16 = jnp.bfloat16

N_MIXERS = 3
GRID_W = 64
NORM_EPS = 1e-6
POOL_WINDOWS = (2, 4, 8, 16)
GLA_HEADS = 4
GLA_TAU = 16.0
GLA_CHUNK = 64
ATTN_HEAD_DIM = 128
ATTN_KV_HEADS = 4
ROPE_THETA = 10000.0

LANES = 128
BF16_SUBLANES = 16
VMEM_LIMIT = 56 * 1024 * 1024
NEG_BIG = -1e30
LOG2E = 1.4426950408889634


def _tile(n, pref, mult):
    t = min(pref, n)
    t -= t % mult
    while t >= mult:
        if n % t == 0:
            return t
        t -= mult
    return n


def _silu(x):
    return x / (1.0 + jnp.exp(-x))


def _params(sem):
    return pltpu.CompilerParams(dimension_semantics=sem, vmem_limit_bytes=VMEM_LIMIT)


def _dot(a, b):
    return jnp.dot(a, b, preferred_element_type=F32)


def _dot_nt(a, b):
    return lax.dot_general(a, b, (((1,), (1,)), ((), ())), preferred_element_type=F32)


def _dot_tn(a, b):
    return lax.dot_general(a, b, (((0,), (0,)), ((), ())), preferred_element_type=F32)


def _dot_f32(a, b):
    return jnp.dot(a, b, preferred_element_type=F32, precision=lax.Precision.HIGHEST)


def _mod_kernel(c_ref, w_ref, b_ref, o_ref):
    sc = _silu(c_ref[...]).astype(BF16)
    o_ref[...] = _dot(sc, w_ref[...].astype(BF16)) + b_ref[...]


def _modulation(c, w_mod, b_mod):
    L, D, D3 = w_mod.shape
    B = c.shape[0]
    rows = -(-B // BF16_SUBLANES) * BF16_SUBLANES
    cp = jnp.pad(c, ((0, rows - B), (0, 0)))
    tn = _tile(D3, 1536, LANES)
    out = pl.pallas_call(
        _mod_kernel,
        out_shape=jax.ShapeDtypeStruct((L, rows, D3), F32),
        grid=(L, D3 // tn),
        in_specs=[
            pl.BlockSpec((rows, D), lambda l, j: (0, 0)),
            pl.BlockSpec((None, D, tn), lambda l, j: (l, 0, j)),
            pl.BlockSpec((None, 1, tn), lambda l, j: (l, 0, j)),
        ],
        out_specs=pl.BlockSpec((None, rows, tn), lambda l, j: (l, 0, j)),
        compiler_params=_params(("parallel", "parallel")),
        name="adaln_modulation",
    )(cp, w_mod, b_mod.reshape(L, 1, D3))
    return out[:, :B]


def _mod_spec(layer, part, tm, T, D, nargs):
    if nargs == 1:
        return pl.BlockSpec((None, None, None, 1, D),
                            lambda i: (layer, (i * tm) // T, part, 0, 0))
    return pl.BlockSpec((None, None, None, 1, D),
                        lambda i, j: (layer, (i * tm) // T, part, 0, 0))


def _norm_mod(x_ref, sh_ref, sc_ref, h_ref):
    xf = x_ref[...]
    ms = jnp.mean(xf * xf, axis=-1, keepdims=True)
    h = xf * lax.rsqrt(ms + NORM_EPS) * (1.0 + sc_ref[...]) + sh_ref[...]
    h_ref[...] = h.astype(BF16)


def _inproj_plain_kernel(x_ref, sh_ref, sc_ref, w_ref, o_ref, h_ref):
    @pl.when(pl.program_id(1) == 0)
    def _():
        _norm_mod(x_ref, sh_ref, sc_ref, h_ref)

    o_ref[...] = _dot(h_ref[...], w_ref[...]).astype(o_ref.dtype)


def _inproj_gla_kernel(x_ref, sh_ref, sc_ref, w_ref, w1_ref, o_ref, r_ref, h_ref, *,
                       q_tiles, q_scale):
    j = pl.program_id(1)

    @pl.when(j == 0)
    def _():
        _norm_mod(x_ref, sh_ref, sc_ref, h_ref)
        r_ref[...] = _dot(h_ref[...], w1_ref[...])

    s = jnp.where(j < q_tiles, q_scale, 1.0).astype(F32)
    o_ref[...] = (_dot(h_ref[...], w_ref[...]) * s).astype(o_ref.dtype)


def _norm_rope(acc, gain, cos, sin, post_scale):
    tn = acc.shape[1]
    lane = lax.broadcasted_iota(jnp.int32, (1, ATTN_HEAD_DIM), 1)
    even = (lane % 2) == 0
    outs = []
    for s in range(tn // ATTN_HEAD_DIM):
        a = acc[:, s * ATTN_HEAD_DIM:(s + 1) * ATTN_HEAD_DIM]
        ms = jnp.mean(a * a, axis=-1, keepdims=True)
        a = a * lax.rsqrt(ms + NORM_EPS) * gain
        partner = jnp.where(even, pltpu.roll(a, ATTN_HEAD_DIM - 1, 1), pltpu.roll(a, 1, 1))
        outs.append((a * cos + partner * sin) * post_scale)
    return jnp.concatenate(outs, axis=1)


def _inproj_attn_kernel(x_ref, sh_ref, sc_ref, w_ref, gq_ref, gk_ref, cos_ref, sin_ref,
                        o_ref, h_ref, *, q_tiles, k_lo, k_hi, q_scale):
    j = pl.program_id(1)

    @pl.when(j == 0)
    def _():
        _norm_mod(x_ref, sh_ref, sc_ref, h_ref)

    acc = _dot(h_ref[...], w_ref[...])
    is_q = j < q_tiles
    is_k = jnp.logical_and(j >= k_lo, j < k_hi)

    @pl.when(is_q)
    def _():
        o_ref[...] = _norm_rope(acc, gq_ref[...], cos_ref[...], sin_ref[...],
                                q_scale).astype(o_ref.dtype)

    @pl.when(is_k)
    def _():
        o_ref[...] = _norm_rope(acc, gk_ref[...], cos_ref[...], sin_ref[...],
                                1.0).astype(o_ref.dtype)

    @pl.when(jnp.logical_not(jnp.logical_or(is_q, is_k)))
    def _():
        o_ref[...] = acc.astype(o_ref.dtype)


def _inproj(x, mod5, layer, w, T, *, mode, extra=None):
    N, D = x.shape
    W = w.shape[1]
    tm = _tile(T, 1024, 8)
    if mode == "attn":
        tn = _tile(extra["kv_w"], 512, LANES)
    elif mode == "gla":
        tn = _tile(extra["key_w"], 1024, LANES)
    else:
        tn = _tile(W, 1024, LANES)
    grid = (N // tm, W // tn)
    in_specs = [
        pl.BlockSpec((tm, D), lambda i, j: (i, 0)),
        _mod_spec(layer, 0, tm, T, D, 2),
        _mod_spec(layer, 1, tm, T, D, 2),
        pl.BlockSpec((D, tn), lambda i, j: (0, j)),
    ]
    args = [x, mod5, mod5, w]
    out_shape = jax.ShapeDtypeStruct((N, W), BF16)
    out_specs = pl.BlockSpec((tm, tn), lambda i, j: (i, j))
    scratch = [pltpu.VMEM((tm, D), BF16)]
    if mode == "plain":
        body = _inproj_plain_kernel
    elif mode == "gla":
        w1 = extra["w1"]
        body = functools.partial(_inproj_gla_kernel, q_tiles=extra["key_w"] // tn,
                                 q_scale=extra["q_scale"])
        assert extra["key_w"] % tn == 0
        in_specs.append(pl.BlockSpec(w1.shape, lambda i, j: (0, 0)))
        args.append(w1)
        out_shape = (out_shape, jax.ShapeDtypeStruct((N, w1.shape[1]), F32))
        out_specs = (out_specs, pl.BlockSpec((tm, w1.shape[1]), lambda i, j: (i, 0)))
    else:
        q_w, g_w, kv_w = extra["q_w"], extra["g_w"], extra["kv_w"]
        assert q_w % tn == 0 and g_w % tn == 0 and kv_w % tn == 0
        body = functools.partial(
            _inproj_attn_kernel, q_tiles=q_w // tn, k_lo=(q_w + g_w) // tn,
            k_hi=(q_w + g_w + kv_w) // tn, q_scale=extra["q_scale"])
        tpb = T // tm
        in_specs += [
            pl.BlockSpec((1, ATTN_HEAD_DIM), lambda i, j: (0, 0)),
            pl.BlockSpec((1, ATTN_HEAD_DIM), lambda i, j: (0, 0)),
            pl.BlockSpec((tm, ATTN_HEAD_DIM), lambda i, j: (i % tpb, 0)),
            pl.BlockSpec((tm, ATTN_HEAD_DIM), lambda i, j: (i % tpb, 0)),
        ]
        args += [extra["gq"], extra["gk"], extra["cos"], extra["sin"]]
    return pl.pallas_call(
        body, out_shape=out_shape, grid=grid, in_specs=in_specs, out_specs=out_specs,
        scratch_shapes=scratch, compiler_params=_params(("parallel", "arbitrary")),
        name="inproj_" + mode,
    )(*args)


def _finish(y_ref, wout_ref, x_ref, gate_ref, fg_ref, o_ref, final):
    out = x_ref[...] + gate_ref[...] * _dot(y_ref[...], wout_ref[...])
    if final:
        ms = jnp.mean(out * out, axis=-1, keepdims=True)
        out = out * lax.rsqrt(ms + NORM_EPS) * fg_ref[...]
    o_ref[...] = out


def _pool_out_kernel(u_ref, up_ref, un_ref, g_ref, x_ref, gate_ref, wg_ref, ps_ref,
                     wout_ref, fg_ref, o_ref, ext_ref, y_ref, *, T, final):
    tm = u_ref.shape[0]
    hb = up_ref.shape[0]
    cg = wg_ref.shape[1]
    tiles_per_seq = T // tm
    ti = pl.program_id(0) % tiles_per_seq
    first = ti == 0
    last = ti == tiles_per_seq - 1
    pos = ti * tm + lax.broadcasted_iota(jnp.int32, (tm, 1), 0)
    for gi, w in enumerate(POOL_WINDOWS):
        cs = slice(gi * cg, (gi + 1) * cg)
        um = u_ref[:, cs].astype(F32)
        ext_ref[0:hb, :] = jnp.where(first, 0.0, up_ref[:, cs].astype(F32))
        ext_ref[hb:hb + tm, :] = um
        ext_ref[hb + tm:hb + tm + hb, :] = jnp.where(last, 0.0, un_ref[:, cs].astype(F32))
        wsum = ext_ref[hb - w // 2:hb - w // 2 + tm, :]
        for k in range(1 - w // 2, w // 2):
            wsum = wsum + ext_ref[hb + k:hb + k + tm, :]
        cnt = (jnp.minimum(pos + w // 2, T) - jnp.maximum(pos - w // 2, 0)).astype(F32)
        pooled = wsum / cnt - um
        yg = _dot(pooled.astype(BF16), wg_ref[gi]) * ps_ref[:, cs]
        y_ref[:, cs] = (yg * _silu(g_ref[:, cs].astype(F32))).astype(BF16)
    _finish(y_ref, wout_ref, x_ref, gate_ref, fg_ref, o_ref, final)


def _gla_out_kernel(of_ref, ob_ref, g_ref, x_ref, gate_ref, ng_ref, wout_ref, fg_ref,
                    o_ref, y_ref, *, final):
    dv = ng_ref.shape[1]
    for h in range(GLA_HEADS):
        hs = slice(h * dv, (h + 1) * dv)
        o = of_ref[:, hs].astype(F32) + ob_ref[:, hs].astype(F32)
        ms = jnp.mean(o * o, axis=-1, keepdims=True)
        o = o * lax.rsqrt(ms + NORM_EPS) * ng_ref[...]
        y_ref[:, hs] = (o * _silu(g_ref[:, hs].astype(F32))).astype(BF16)
    _finish(y_ref, wout_ref, x_ref, gate_ref, fg_ref, o_ref, final)


def _attn_out_kernel(a_ref, g_ref, x_ref, gate_ref, wout_ref, fg_ref, o_ref, y_ref, *, final):
    y_ref[...] = (a_ref[...].astype(F32) * _silu(g_ref[...].astype(F32))).astype(BF16)
    _finish(y_ref, wout_ref, x_ref, gate_ref, fg_ref, o_ref, final)


def _const_spec(shape):
    nd = len(shape)
    return pl.BlockSpec(shape, lambda i: (0,) * nd)


def _out_call(body, name, pre_args, pre_specs, post_args, post_specs, x, mod5, layer, T,
              w_out, final_g, tm, scratch):
    N, D = x.shape
    bw = w_out.shape[0]
    in_specs = (pre_specs
                + [pl.BlockSpec((tm, D), lambda i: (i, 0)), _mod_spec(layer, 2, tm, T, D, 1)]
                + post_specs + [_const_spec((bw, D)), _const_spec((1, D))])
    args = pre_args + [x, mod5] + post_args + [w_out, final_g.reshape(1, D)]
    return pl.pallas_call(
        body, out_shape=jax.ShapeDtypeStruct((N, D), F32), grid=(N // tm,),
        in_specs=in_specs, out_specs=pl.BlockSpec((tm, D), lambda i: (i, 0)),
        scratch_shapes=scratch + [pltpu.VMEM((tm, bw), BF16)],
        compiler_params=_params(("parallel",)), name=name,
    )(*args)


def _pool_out(ug, x, mod5, layer, T, w_grp, scale, w_out, final_g, final):
    N, D = x.shape
    bw = w_out.shape[0]
    cg = w_grp.shape[1]
    hb = BF16_SUBLANES
    assert max(POOL_WINDOWS) // 2 <= hb
    tm = _tile(T, 256, hb)
    r = tm // hb
    nhb = N // hb
    pre_specs = [
        pl.BlockSpec((tm, bw), lambda i: (i, 0)),
        pl.BlockSpec((hb, bw), lambda i: (jnp.maximum(i * r - 1, 0), 0)),
        pl.BlockSpec((hb, bw), lambda i: (jnp.minimum((i + 1) * r, nhb - 1), 0)),
        pl.BlockSpec((tm, bw), lambda i: (i, 1)),
    ]
    post_specs = [_const_spec(w_grp.shape), _const_spec((1, bw))]
    body = functools.partial(_pool_out_kernel, T=T, final=final)
    return _out_call(body, "pool_out", [ug, ug, ug, ug], pre_specs,
                     [w_grp, scale.reshape(1, bw)], post_specs, x, mod5, layer, T, w_out,
                     final_g, tm, [pltpu.VMEM((tm + 2 * hb, cg), F32)])


def _gla_out(o_f, o_b, qkvg, x, mod5, layer, T, norm_g, w_out, final_g, final):
    N, D = x.shape
    bw = w_out.shape[0]
    tm = _tile(T, 512, 8)
    pre_specs = [
        pl.BlockSpec((tm, bw), lambda i: (i, 0)),
        pl.BlockSpec((tm, bw), lambda i: (i, 0)),
        pl.BlockSpec((tm, bw), lambda i: (i, 2)),
    ]
    dv = norm_g.shape[0]
    body = functools.partial(_gla_out_kernel, final=final)
    return _out_call(body, "gla_out", [o_f, o_b, qkvg], pre_specs, [norm_g.reshape(1, dv)],
                     [_const_spec((1, dv))], x, mod5, layer, T, w_out, final_g, tm, [])


def _attn_out(a, qgkv, x, mod5, layer, T, w_out, final_g, final):
    N, D = x.shape
    bw = w_out.shape[0]
    tm = _tile(T, 512, 8)
    pre_specs = [
        pl.BlockSpec((tm, bw), lambda i: (i, 0)),
        pl.BlockSpec((tm, bw), lambda i: (i, 1)),
    ]
    body = functools.partial(_attn_out_kernel, final=final)
    return _out_call(body, "attn_out", [a, qgkv], pre_specs, [], [], x, mod5, layer, T,
                     w_out, final_g, tm, [])


def _block_ref_rows(cum, m, ref_local):
    C, W = cum.shape
    blk = 2 * m
    if blk >= 8:
        x3 = cum.reshape(C // blk, blk, W)
        r = x3[:, ref_local:ref_local + 1, :]
        return jnp.broadcast_to(r, x3.shape).reshape(C, W)
    x3 = cum.reshape(C // 8, 8, W)
    sub = lax.broadcasted_iota(jnp.int32, (1, 8, 1), 1)
    out = None
    for p in range(8 // blk):
        r = jnp.broadcast_to(x3[:, p * blk + ref_local:p * blk + ref_local + 1, :], x3.shape)
        out = r if out is None else jnp.where(sub >= p * blk, r, out)
    return out.reshape(C, W)


def _gla_chunk(qk_ref, v_ref, r_ref, w2_ref, b_ref, st_ref, o_ref, off, fwd):
    C = GLA_CHUNK
    H = GLA_HEADS
    kw = b_ref.shape[1]
    dk = kw // H
    dv = v_ref.shape[1] // H
    rows = pl.ds(off, C)
    row = lax.broadcasted_iota(jnp.int32, (C, 1), 0)
    rr = lax.broadcasted_iota(jnp.int32, (C, C), 0)
    cc = lax.broadcasted_iota(jnp.int32, (C, C), 1)

    z = _dot_f32(r_ref[rows, :], w2_ref[...]) + b_ref[...]
    la = (jnp.minimum(z, 0.0) - jnp.log(1.0 + jnp.exp(-jnp.abs(z)))) * (1.0 / GLA_TAU)
    tri = (rr >= cc) if fwd else (cc >= rr)
    cum = _dot_f32(tri.astype(F32), la)
    tot = cum[C - 1:C, :] if fwd else cum[0:1, :]

    q = qk_ref[rows, 0:kw].astype(F32)
    k = qk_ref[rows, kw:2 * kw].astype(F32)
    q_inter = (q * jnp.exp(cum)).astype(BF16)
    k_dec = (k * jnp.exp(tot - cum)).astype(BF16)
    dec = jnp.exp(tot)

    amat = [jnp.zeros((C, C), F32) for _ in range(H)]
    m = C // 2
    while m >= 1:
        ref = _block_ref_rows(cum, m, m - 1 if fwd else m)
        upper = ((row // m) % 2) == 1
        qmask = upper if fwd else jnp.logical_not(upper)
        qm = (q * jnp.exp(jnp.where(qmask, cum - ref, NEG_BIG))).astype(BF16)
        km = (k * jnp.exp(jnp.where(qmask, NEG_BIG, ref - cum))).astype(BF16)
        same = (rr // (2 * m)) == (cc // (2 * m))
        for h in range(H):
            hs = slice(h * dk, (h + 1) * dk)
            amat[h] = amat[h] + jnp.where(same, _dot_nt(qm[:, hs], km[:, hs]), 0.0)
        m //= 2
    if fwd:
        qb = q.astype(BF16)
        kb = k.astype(BF16)
        for h in range(H):
            hs = slice(h * dk, (h + 1) * dk)
            amat[h] = amat[h] + jnp.where(rr == cc, _dot_nt(qb[:, hs], kb[:, hs]), 0.0)

    for h in range(H):
        hs = slice(h * dk, (h + 1) * dk)
        vs = slice(h * dv, (h + 1) * dv)
        v = v_ref[rows, vs]
        st = st_ref[h]
        o = _dot_nt(q_inter[:, hs], st.astype(BF16)) + _dot(amat[h].astype(BF16), v)
        o_ref[rows, vs] = o.astype(o_ref.dtype)
        st_ref[h] = st * dec[:, hs] + _dot_tn(v, k_dec[:, hs])


def _gla_kernel(qkf_ref, vf_ref, rf_ref, qkb_ref, vb_ref, rb_ref, w2f_ref, w2b_ref,
                bf_ref, bb_ref, of_ref, ob_ref, sf_ref, sb_ref):
    @pl.when(pl.program_id(1) == 0)
    def _():
        sf_ref[...] = jnp.zeros_like(sf_ref)
        sb_ref[...] = jnp.zeros_like(sb_ref)

    nsub = qkf_ref.shape[0] // GLA_CHUNK

    def body(ci, carry):
        off_f = pl.multiple_of(ci * GLA_CHUNK, GLA_CHUNK)
        off_b = pl.multiple_of((nsub - 1 - ci) * GLA_CHUNK, GLA_CHUNK)
        _gla_chunk(qkf_ref, vf_ref, rf_ref, w2f_ref, bf_ref, sf_ref, of_ref, off_f, True)
        _gla_chunk(qkb_ref, vb_ref, rb_ref, w2b_ref, bb_ref, sb_ref, ob_ref, off_b, False)
        return carry

    lax.fori_loop(0, nsub, body, 0)


def _gla_core(qkvg, r, w2f, w2b, bias_f, bias_b, B, T, D):
    kw = bias_f.shape[1]
    H = GLA_HEADS
    tsup = _tile(T, 256, GLA_CHUNK)
    ns = T // tsup
    rw = r.shape[-1]
    qkvg3 = qkvg.reshape(B, T, qkvg.shape[-1])
    r3 = r.reshape(B, T, rw)
    assert 2 * kw == D
    fw = lambda b, s: (b, s, 0)
    fw1 = lambda b, s: (b, s, 1)
    bw = lambda b, s: (b, ns - 1 - s, 0)
    bw1 = lambda b, s: (b, ns - 1 - s, 1)
    cst = lambda b, s: (0, 0)
    in_specs = [
        pl.BlockSpec((None, tsup, D), fw), pl.BlockSpec((None, tsup, D), fw1),
        pl.BlockSpec((None, tsup, rw), fw),
        pl.BlockSpec((None, tsup, D), bw), pl.BlockSpec((None, tsup, D), bw1),
        pl.BlockSpec((None, tsup, rw), bw),
        pl.BlockSpec((rw, kw), cst), pl.BlockSpec((rw, kw), cst),
        pl.BlockSpec((1, kw), cst), pl.BlockSpec((1, kw), cst),
    ]
    out_shape = (jax.ShapeDtypeStruct((B, T, D), BF16), jax.ShapeDtypeStruct((B, T, D), BF16))
    out_specs = (pl.BlockSpec((None, tsup, D), fw), pl.BlockSpec((None, tsup, D), bw))
    st = pltpu.VMEM((H, D // H, kw // H), F32)
    o_f, o_b = pl.pallas_call(
        _gla_kernel, out_shape=out_shape, grid=(B, ns), in_specs=in_specs,
        out_specs=out_specs, scratch_shapes=[st, st],
        compiler_params=_params(("parallel", "arbitrary")), name="gla_scan",
    )(qkvg3, qkvg3, r3, qkvg3, qkvg3, r3, w2f, w2b, bias_f, bias_b)
    return o_f.reshape(B * T, D), o_b.reshape(B * T, D)


def _attn_kernel(q_ref, k_ref, v_ref, o_ref, vt_ref, acc_ref, s0_ref, s1_ref, *, group):
    tq = q_ref.shape[0]
    hd = ATTN_HEAD_DIM
    nk, _, tk = vt_ref.shape

    @pl.when(pl.program_id(2) == 0)
    def _():
        for c in range(nk):
            vt_ref[c] = v_ref[c * tk:(c + 1) * tk, :].T

    qs = jnp.concatenate([q_ref[:, g * hd:(g + 1) * hd] for g in range(group)], axis=0)
    rows = qs.shape[0]
    acc_ref[...] = jnp.zeros_like(acc_ref)
    s_bufs = (s0_ref, s1_ref)

    def scores(j, buf):
        st = _dot_nt(k_ref[pl.ds(pl.multiple_of(j * tk, tk), tk), :], qs)
        buf[...] = st
        return jnp.max(st, axis=0, keepdims=True)

    def consume(j, buf, m_blk, m_prev, l_prev):
        m_new = jnp.maximum(m_prev, m_blk)
        alpha = jnp.exp2(m_prev - m_new)
        p = jnp.exp2(buf[...] - m_new)
        l_new = alpha * l_prev + jnp.sum(p, axis=0, keepdims=True)
        acc_ref[...] = alpha * acc_ref[...] + _dot(vt_ref[j], p.astype(BF16))
        return m_new, l_new

    def body(jj, carry):
        m_prev, l_prev, m_blk = carry
        for u in range(2):
            j = jj * 2 + u
            m_next = scores(jnp.minimum(j + 1, nk - 1), s_bufs[1 - u])
            m_prev, l_prev = consume(j, s_bufs[u], m_blk, m_prev, l_prev)
            m_blk = m_next
        return m_prev, l_prev, m_blk

    init = (jnp.full((1, rows), -jnp.inf, F32), jnp.zeros((1, rows), F32),
            scores(0, s_bufs[0]))
    _, l_fin, _ = lax.fori_loop(0, nk // 2, body, init)
    o = (acc_ref[...] / l_fin).T
    o_ref[...] = jnp.concatenate([o[g * tq:(g + 1) * tq, :] for g in range(group)],
                                 axis=1).astype(o_ref.dtype)


def _attention(qgkv, B, T, q_w, g_w):
    hd = ATTN_HEAD_DIM
    kvh = ATTN_KV_HEADS
    group = q_w // (kvh * hd)
    gw = group * hd
    tq = _tile(T, 128, 8)
    tk = _tile(T // 2, 512, LANES)
    nk = T // tk
    rows = group * tq
    k_blk = (q_w + g_w) // hd
    v_blk = k_blk + kvh
    a3 = qgkv.reshape(B, T, qgkv.shape[-1])
    out = pl.pallas_call(
        functools.partial(_attn_kernel, group=group),
        out_shape=jax.ShapeDtypeStruct((B, T, q_w), BF16),
        grid=(B, kvh, T // tq),
        in_specs=[
            pl.BlockSpec((None, tq, gw), lambda b, h, i: (b, i, h)),
            pl.BlockSpec((None, T, hd), lambda b, h, i: (b, 0, k_blk + h)),
            pl.BlockSpec((None, T, hd), lambda b, h, i: (b, 0, v_blk + h)),
        ],
        out_specs=pl.BlockSpec((None, tq, gw), lambda b, h, i: (b, i, h)),
        scratch_shapes=[pltpu.VMEM((nk, hd, tk), BF16), pltpu.VMEM((hd, rows), F32),
                        pltpu.VMEM((tk, rows), F32), pltpu.VMEM((tk, rows), F32)],
        compiler_params=_params(("parallel", "parallel", "arbitrary")),
        name="gqa_flash",
    )(a3, a3, a3)
    return out.reshape(B * T, q_w)


def _rope_tables(T):
    rows = T // GRID_W
    t = jnp.arange(T)
    row = (t // GRID_W - rows // 2).astype(F32)
    col = (t % GRID_W - GRID_W // 2).astype(F32)
    axis = ATTN_HEAD_DIM // 2
    inv = ROPE_THETA ** (-jnp.arange(0, axis, 2, dtype=F32) / axis)
    ang = jnp.concatenate([row[:, None] * inv, col[:, None] * inv], axis=-1)
    cos = jnp.repeat(jnp.cos(ang), 2, axis=-1)
    sin = jnp.repeat(jnp.sin(ang), 2, axis=-1)
    sign = jnp.where(jnp.arange(ATTN_HEAD_DIM) % 2 == 0, -1.0, 1.0).astype(F32)
    return cos, sin * sign


def kernel(x, c, w_mod, b_mod, pool_w_in, pool_w_grp, pool_scale, pool_w_out, gla_w_in, gla_fwd_w1, gla_fwd_w2, gla_fwd_b, gla_bwd_w1, gla_bwd_w2, gla_bwd_b, gla_norm_g, gla_w_out, attn_w_in, attn_q_norm_g, attn_k_norm_g, attn_w_out, final_norm_g):
    B, T, D = x.shape
    depth = w_mod.shape[0]
    mod5 = _modulation(c, w_mod, b_mod).reshape(depth, B, 3, 1, D)
    xf = x.reshape(B * T, D)
    for i in range(depth):
        kind, j = i % N_MIXERS, i // N_MIXERS
        final = i == depth - 1
        if kind == 0:
            ug = _inproj(xf, mod5, i, pool_w_in[j].astype(BF16), T, mode="plain")
            xf = _pool_out(ug, xf, mod5, i, T, pool_w_grp[j].astype(BF16), pool_scale[j],
                           pool_w_out[j].astype(BF16), final_norm_g, final)
        elif kind == 1:
            key_w = gla_fwd_b.shape[1]
            lr = gla_fwd_w1.shape[2]
            w1 = jnp.concatenate([gla_fwd_w1[j], gla_bwd_w1[j]], axis=1)
            w1 = jnp.pad(w1, ((0, 0), (0, LANES - 2 * lr))).astype(BF16)
            w2f = jnp.pad(gla_fwd_w2[j], ((0, LANES - lr), (0, 0)))
            w2b = jnp.pad(gla_bwd_w2[j], ((lr, LANES - 2 * lr), (0, 0)))
            extra = dict(w1=w1, key_w=key_w, q_scale=float(key_w // GLA_HEADS) ** -0.5)
            qkvg, r = _inproj(xf, mod5, i, gla_w_in[j].astype(BF16), T, mode="gla",
                              extra=extra)
            o_f, o_b = _gla_core(qkvg, r, w2f, w2b, gla_fwd_b[j].reshape(1, key_w),
                                 gla_bwd_b[j].reshape(1, key_w), B, T, D)
            xf = _gla_out(o_f, o_b, qkvg, xf, mod5, i, T, gla_norm_g[j],
                          gla_w_out[j].astype(BF16), final_norm_g, final)
        else:
            q_w = attn_w_out.shape[1]
            kv_w = ATTN_KV_HEADS * ATTN_HEAD_DIM
            w = attn_w_in[j]
            w = jnp.concatenate([w[:, :q_w], w[:, q_w + 2 * kv_w:], w[:, q_w:q_w + 2 * kv_w]],
                                axis=1).astype(BF16)
            cos, sin = _rope_tables(T)
            extra = dict(q_w=q_w, g_w=D, kv_w=kv_w, q_scale=LOG2E * float(ATTN_HEAD_DIM) ** -0.5,
                         gq=attn_q_norm_g[j].reshape(1, -1), gk=attn_k_norm_g[j].reshape(1, -1),
                         cos=cos, sin=sin)
            qgkv = _inproj(xf, mod5, i, w, T, mode="attn", extra=extra)
            a = _attention(qgkv, B, T, q_w, D)
            xf = _attn_out(a, qgkv, xf, mod5, i, T, attn_w_out[j].astype(BF16),
                           final_norm_g, final)
    return xf.reshape(B, T, D)
```
